```python
import jax, jax.numpy as jnp
from jax import lax
import numpy as np

D_MODEL = 1024
BATCH = 4
SEQ = 4096
DEPTH = 1
DEC_BATCH = 128
DEC_SEQ = 8
PAST_LEN = 2048
PAGE_SIZE = 128

D_RNN = D_MODEL
LRU_BLOCKS = 16
LRU_BLOCK = D_RNN // LRU_BLOCKS
LRU_CONV = 4
LRU_C = 8.0
ATT_PATTERNS = ((128, 1), (512, 4), (2048, 16))
N_GROUPS = len(ATT_PATTERNS)
HEADS_PER_GROUP = 4
HEAD_DIM = 128
ATT_W = N_GROUPS * HEADS_PER_GROUP * HEAD_DIM
ATT_OUT = HEADS_PER_GROUP * HEAD_DIM
ATT_Q_BLOCK = 128
D_FF = 3 * D_MODEL
FFN_CONV = 3
N_IN = 2 * D_RNN + 3 * ATT_W + 2 * D_MODEL
IN_SPLITS = (D_RNN, 2 * D_RNN, 2 * D_RNN + ATT_W, 2 * D_RNN + 2 * ATT_W,
             2 * D_RNN + 3 * ATT_W, 2 * D_RNN + 3 * ATT_W + D_MODEL)
NORM_EPS = 1e-6
NEG_BIG = -1e30

kernel_name = 'hawk_dilated_swa_convffn_step'


def rmsnorm(x, g):
    xf = x.astype(jnp.float32)
    y = xf * lax.rsqrt(jnp.mean(xf * xf, axis=-1, keepdims=True) + NORM_EPS) * g.astype(jnp.float32)
    return y.astype(x.dtype)


def causal_dwconv(u, buf, w, b):
    width = w.shape[0]
    T = u.shape[1]
    ext = jnp.concatenate([buf.astype(u.dtype), u], axis=1)
    out = b + ext[:, 0:T] * w[0]
    for i in range(1, width):
        out = out + ext[:, i:i + T] * w[i]
    return out, ext[:, T:]


def rg_lru(xc, pos, h0, wa, ba, wx, bx, lam):
    B, T, D = xc.shape
    xf = xc.astype(jnp.float32)
    xb = xf.reshape(B, T, LRU_BLOCKS, LRU_BLOCK)
    r = jax.nn.sigmoid(jnp.einsum('btnd,nde->btne', xb, wa).reshape(B, T, D) + ba)
    i = jax.nn.sigmoid(jnp.einsum('btnd,nde->btne', xb, wx).reshape(B, T, D) + bx)
    log_a = -LRU_C * r * jax.nn.softplus(-lam.astype(jnp.float32))
    a = jnp.exp(log_a)
    mult = jnp.sqrt(-jnp.expm1(2.0 * log_a))
    mult = jnp.where((pos == 0)[None, :, None], 1.0, mult)
    u = mult * i * xf

    def step(h, au):
        a_t, u_t = au
        h = a_t * h + u_t
        return h, h

    h_T, hs = lax.scan(step, h0.astype(jnp.float32),
                       (jnp.swapaxes(a, 0, 1), jnp.swapaxes(u, 0, 1)))
    return jnp.swapaxes(hs, 0, 1).astype(xc.dtype), h_T.astype(xc.dtype)


def dilated_attn(q, k_ext, v_ext, q_pos, ext_start, window, dilation):
    B, Tq, H, Dh = q.shape
    L = k_ext.shape[1]
    offs = jnp.arange(window // dilation + 1, dtype=jnp.int32) * dilation
    qb = ATT_Q_BLOCK if Tq % ATT_Q_BLOCK == 0 else Tq
    nb = Tq // qb
    scale = HEAD_DIM ** -0.5

    def block(args):
        q_blk, pos_blk = args
        key_pos = pos_blk[:, None] - offs[None, :]
        valid = key_pos >= 0
        idx = jnp.clip(key_pos - ext_start, 0, L - 1)
        kg = k_ext[:, idx].astype(jnp.float32)
        vg = v_ext[:, idx].astype(jnp.float32)
        s = jnp.einsum('bqhd,bqjhd->bqhj', q_blk.astype(jnp.float32), kg) * scale
        s = jnp.where(valid[None, :, None, :], s, NEG_BIG)
        m = jnp.max(s, axis=-1, keepdims=True)
        p = jnp.exp(s - m)
        den = jnp.sum(p, axis=-1)
        o = jnp.einsum('bqhj,bqjhd->bqhd', p, vg) / den[..., None]
        return o, m[..., 0] + jnp.log(den)

    q_blocks = jnp.moveaxis(q.reshape(B, nb, qb, H, Dh), 1, 0)
    o, lse = lax.map(block, (q_blocks, q_pos.reshape(nb, qb)))
    o = jnp.moveaxis(o, 0, 1).reshape(B, Tq, H, Dh)
    lse = jnp.moveaxis(lse, 0, 1).reshape(B, Tq, H)
    return o, lse


def ring_to_chrono(buf, past_len):
    n = buf.shape[1]
    return jnp.roll(buf, -((past_len - n) % n), axis=1)


def chrono_to_ring(rows, end_pos):
    n = rows.shape[1]
    return jnp.roll(rows, (end_pos - n) % n, axis=1)


def hybrid_layer(x, pos0, lru_h0, lru_conv0, kv_past, ffn_conv0,
                 norm1_g, w_in, conv_a_w, conv_a_b, lru_wa, lru_ba, lru_wx, lru_bx, lru_lambda,
                 q_norm_g, k_norm_g, w_br_a, w_br_b, w_out, norm2_g, w_up, conv_f_w, conv_f_b, w_down):
    B, T, _ = x.shape
    pos = pos0 + jnp.arange(T, dtype=jnp.int32)
    h = rmsnorm(x, norm1_g)
    xa, ga, q, k, v, gate_a, gate_b = jnp.split(jnp.einsum('btd,de->bte', h, w_in), IN_SPLITS, axis=-1)

    xc, new_conv_a = causal_dwconv(xa, lru_conv0, conv_a_w, conv_a_b)
    lru_y, new_h = rg_lru(xc, pos, lru_h0, lru_wa, lru_ba, lru_wx, lru_bx, lru_lambda)
    branch_a = jnp.einsum('btc,cd->btd', lru_y * jax.nn.gelu(ga), w_br_a)

    shp = (B, T, N_GROUPS, HEADS_PER_GROUP, HEAD_DIM)
    q = rmsnorm(q.reshape(shp), q_norm_g[:, None, :])
    k = rmsnorm(k.reshape(shp), k_norm_g[:, None, :])
    v = v.reshape(shp)
    outs, lses, new_kv = [], [], []
    for g, (window, dilation) in enumerate(ATT_PATTERNS):
        qg, kg, vg = q[:, :, g], k[:, :, g], v[:, :, g]
        if kv_past is None:
            k_ext, v_ext, ext_start = kg, vg, pos0
            keep = min(window, T)
            new_kv += [chrono_to_ring(kg[:, T - keep:], pos0 + T),
                       chrono_to_ring(vg[:, T - keep:], pos0 + T)]
        else:
            k_past, v_past = kv_past[g]
            k_ext = jnp.concatenate([k_past.astype(kg.dtype), kg], axis=1)
            v_ext = jnp.concatenate([v_past.astype(vg.dtype), vg], axis=1)
            ext_start = pos0 - k_past.shape[1]
            new_kv += [kg, vg]
        o, lse = dilated_attn(qg, k_ext, v_ext, pos, ext_start, window, dilation)
        outs.append(o)
        lses.append(lse)
    wts = jax.nn.softmax(jnp.stack(lses, axis=0), axis=0)
    att = jnp.sum(wts[..., None] * jnp.stack(outs, axis=0), axis=0)
    att = att.reshape(B, T, ATT_OUT).astype(x.dtype)
    branch_b = jnp.einsum('btc,cd->btd', att, w_br_b)

    merged = jax.nn.sigmoid(gate_a) * branch_a + jax.nn.sigmoid(gate_b) * branch_b
    x = x + jnp.einsum('btd,de->bte', merged, w_out)

    h2 = rmsnorm(x, norm2_g)
    up, new_conv_f = causal_dwconv(jnp.einsum('btd,df->btf', h2, w_up), ffn_conv0, conv_f_w, conv_f_b)
    gf, vf = jnp.split(up, 2, axis=-1)
    x = x + jnp.einsum('btf,fd->btd', jax.nn.gelu(gf) * vf, w_down)
    return x, (new_h, new_conv_a, new_kv[0], new_kv[1], new_kv[2], new_kv[3], new_kv[4], new_kv[5], new_conv_f)


def setup_inputs(seed: int = 0) -> dict:
    key = jax.random.key(seed)
    ks = jax.random.split(key, 32)

    def nrm(k, shape, scale=1.0):
        return scale * jax.random.normal(k, shape, jnp.float32)

    lw = [min(w, PAST_LEN) for w, _ in ATT_PATTERNS]
    kv_shape = lambda n: (DEPTH, DEC_BATCH, n, HEADS_PER_GROUP, HEAD_DIM)
    a0 = jax.random.uniform(ks[30], (DEPTH, D_RNN), jnp.float32, minval=0.9, maxval=0.999)
    return {
        'x_prompt': nrm(ks[0], (BATCH, SEQ, D_MODEL)),
        'x_sample': nrm(ks[1], (DEC_BATCH, DEC_SEQ, D_MODEL)),
        'state_lru_h': nrm(ks[2], (DEPTH, DEC_BATCH, D_RNN), 0.5),
        'state_lru_conv': nrm(ks[3], (DEPTH, DEC_BATCH, LRU_CONV - 1, D_RNN)),
        'cache_k_g0': nrm(ks[4], kv_shape(lw[0])),
        'cache_v_g0': nrm(ks[5], kv_shape(lw[0])),
        'cache_k_g1': nrm(ks[6], kv_shape(lw[1])),
        'cache_v_g1': nrm(ks[7], kv_shape(lw[1])),
        'cache_k_g2': nrm(ks[8], kv_shape(lw[2])),
        'cache_v_g2': nrm(ks[9], kv_shape(lw[2])),
        'state_ffn_conv': nrm(ks[10], (DEPTH, DEC_BATCH, FFN_CONV - 1, 2 * D_FF)),
        'norm1_g': 1.0 + nrm(ks[11], (DEPTH, D_MODEL), 0.1),
        'w_in': nrm(ks[12], (DEPTH, D_MODEL, N_IN), D_MODEL ** -0.5),
        'conv_a_w': nrm(ks[13], (DEPTH, LRU_CONV, D_RNN), LRU_CONV ** -0.5),
        'conv_a_b': nrm(ks[14], (DEPTH, D_RNN), 0.1),
        'lru_wa': nrm(ks[15], (DEPTH, LRU_BLOCKS, LRU_BLOCK, LRU_BLOCK), LRU_BLOCK ** -0.5),
        'lru_ba': nrm(ks[16], (DEPTH, D_RNN), 0.1),
        'lru_wx': nrm(ks[17], (DEPTH, LRU_BLOCKS, LRU_BLOCK, LRU_BLOCK), LRU_BLOCK ** -0.5),
        'lru_bx': nrm(ks[18], (DEPTH, D_RNN), 0.1),
        'lru_lambda': jnp.log(a0) - jnp.log1p(-a0),
        'q_norm_g': 1.0 + nrm(ks[19], (DEPTH, N_GROUPS, HEAD_DIM), 0.1),
        'k_norm_g': 1.0 + nrm(ks[20], (DEPTH, N_GROUPS, HEAD_DIM), 0.1),
        'w_br_a': nrm(ks[21], (DEPTH, D_RNN, D_MODEL), D_RNN ** -0.5),
        'w_br_b': nrm(ks[22], (DEPTH, ATT_OUT, D_MODEL), ATT_OUT ** -0.5),
        'w_out': nrm(ks[23], (DEPTH, D_MODEL, D_MODEL), D_MODEL ** -0.5),
        'norm2_g': 1.0 + nrm(ks[24], (DEPTH, D_MODEL), 0.1),
        'w_up': nrm(ks[25], (DEPTH, D_MODEL, 2 * D_FF), D_MODEL ** -0.5),
        'conv_f_w': nrm(ks[26], (DEPTH, FFN_CONV, 2 * D_FF), FFN_CONV ** -0.5),
        'conv_f_b': nrm(ks[27], (DEPTH, 2 * D_FF), 0.1),
        'w_down': nrm(ks[28], (DEPTH, D_FF, D_MODEL), D_FF ** -0.5),
    }


def reference(x_prompt, x_sample, state_lru_h, state_lru_conv, cache_k_g0, cache_v_g0,
              cache_k_g1, cache_v_g1, cache_k_g2, cache_v_g2, state_ffn_conv,
              norm1_g, w_in, conv_a_w, conv_a_b, lru_wa, lru_ba, lru_wx, lru_bx, lru_lambda,
              q_norm_g, k_norm_g, w_br_a, w_br_b, w_out, norm2_g, w_up, conv_f_w, conv_f_b, w_down):
    yp, ys = x_prompt, x_sample
    bp = x_prompt.shape[0]
    p_states = [[] for _ in range(9)]
    s_states = [[] for _ in range(9)]
    for l in range(DEPTH):
        params = (norm1_g[l], w_in[l], conv_a_w[l], conv_a_b[l], lru_wa[l], lru_ba[l], lru_wx[l],
                  lru_bx[l], lru_lambda[l], q_norm_g[l], k_norm_g[l], w_br_a[l], w_br_b[l], w_out[l],
                  norm2_g[l], w_up[l], conv_f_w[l], conv_f_b[l], w_down[l])
        yp, st_p = hybrid_layer(yp, 0,
                                jnp.zeros((bp, D_RNN), yp.dtype),
                                jnp.zeros((bp, LRU_CONV - 1, D_RNN), yp.dtype),
                                None,
                                jnp.zeros((bp, FFN_CONV - 1, 2 * D_FF), yp.dtype),
                                *params)
        past = ((ring_to_chrono(cache_k_g0[l], PAST_LEN), ring_to_chrono(cache_v_g0[l], PAST_LEN)),
                (ring_to_chrono(cache_k_g1[l], PAST_LEN), ring_to_chrono(cache_v_g1[l], PAST_LEN)),
                (ring_to_chrono(cache_k_g2[l], PAST_LEN), ring_to_chrono(cache_v_g2[l], PAST_LEN)))
        ys, st_s = hybrid_layer(ys, PAST_LEN, state_lru_h[l], state_lru_conv[l], past,
                                state_ffn_conv[l], *params)
        for i in range(9):
            p_states[i].append(st_p[i])
            s_states[i].append(st_s[i])
    (p_lru_h, p_lru_conv, p_k0, p_v0, p_k1, p_v1, p_k2, p_v2, p_ffn_conv) = [jnp.stack(s, axis=0) for s in p_states]
    (s_lru_h, s_lru_conv, s_k0, s_v0, s_k1, s_v1, s_k2, s_v2, s_ffn_conv) = [jnp.stack(s, axis=0) for s in s_states]
    return (yp, ys,
            p_lru_h, p_lru_conv, p_k0, p_v0, p_k1, p_v1, p_k2, p_v2, p_ffn_conv,
            s_lru_h, s_lru_conv, s_k0, s_v0, s_k1, s_v1, s_k2, s_v2, s_ffn_conv)
```

```python
import functools
import math

import jax
import jax.numpy as jnp
from jax import lax
from jax.experimental import pallas as pl
from jax.experimental.pallas import tpu as pltpu

LRU_BLOCKS = 16
LRU_C = 8.0
ATT_PATTERNS = ((128, 1), (512, 4), (2048, 16))
N_GROUPS = len(ATT_PATTERNS)
HEADS = 4
HEAD_DIM = 128
GROUP_W = HEADS * HEAD_DIM
ATT_W = N_GROUPS * GROUP_W
PAST_LEN = 2048
NORM_EPS = 1e-6
NEG_BIG = -1e30

SUBLANES = 8
LANES = 128
MXU_DIM = 256
VMEM_LIMIT = 56 * 1024 * 1024

BF16 = jnp.bfloat16
F32 = jnp.float32


def _dot(a, b):
    return jnp.dot(a, b, preferred_element_type=F32)


def _dot_nt(a, b):
    return lax.dot_general(a, b, (((1,), (1,)), ((), ())), preferred_element_type=F32)


def _rmsnorm(x, g):
    ms = jnp.mean(x * x, axis=-1, keepdims=True)
    return x * lax.rsqrt(ms + NORM_EPS) * g


def _gelu(x):
    c = math.sqrt(2.0 / math.pi)
    return 0.5 * x * (1.0 + jnp.tanh(c * (x + 0.044715 * (x * x * x))))


def _sigmoid(x):
    return 1.0 / (1.0 + jnp.exp(-x))


def _softplus(x):
    return jnp.maximum(x, 0.0) + jnp.log1p(jnp.exp(-jnp.abs(x)))


def _head_norm(x, gains):
    outs = []
    for g in range(N_GROUPS):
        gg = gains[g:g + 1, :]
        for h in range(HEADS):
            c0 = g * GROUP_W + h * HEAD_DIM
            outs.append(_rmsnorm(x[:, c0:c0 + HEAD_DIM], gg))
    return outs


def _lru_gates(xc, g_pre_r, g_pre_i, ba, bx, lam):
    r = _sigmoid(g_pre_r + ba)
    i = _sigmoid(g_pre_i + bx)
    log_a = (-LRU_C) * r * _softplus(-lam)
    a = jnp.exp(log_a)
    mult = jnp.sqrt(1.0 - a * a)
    return a, mult, i * xc


def _shift_rows(x, prev8, s):
    if s == 0:
        return x
    full = pltpu.roll(x, s, 0)
    row8 = lax.broadcasted_iota(jnp.int32, prev8.shape, 0)
    head = jnp.where(row8 < s, pltpu.roll(prev8, s, 0), full[0:SUBLANES])
    return jnp.concatenate([head, full[SUBLANES:]], axis=0)


def _const_spec(shape):
    nd = len(shape)
    return pl.BlockSpec(shape, lambda *_: (0,) * nd, pipeline_mode=pl.Buffered(1))


def _stage1_tail(hb, lru_y, win_ref, wbra_ref, qg_ref, kg_ref, store, d_model):
    c = d_model
    ga = _dot(hb, win_ref[:, c:2 * c])
    ya = (lru_y * _gelu(ga)).astype(BF16)
    br_a = _dot(ya, wbra_ref[...])
    o_q, o_k, o_v = 2 * c, 2 * c + ATT_W, 2 * c + 2 * ATT_W
    o_ga, o_gb = 2 * c + 3 * ATT_W, 3 * c + 3 * ATT_W
    gate_a = _dot(hb, win_ref[:, o_ga:o_ga + c])
    store("outa", slice(0, c), _sigmoid(gate_a) * br_a)
    gate_b = _dot(hb, win_ref[:, o_gb:o_gb + c])
    store("sgb", slice(0, c), _sigmoid(gate_b))
    scale = HEAD_DIM ** -0.5
    q = _dot(hb, win_ref[:, o_q:o_q + ATT_W])
    for j, qh in enumerate(_head_norm(q, qg_ref[...])):
        store("q", slice(j * HEAD_DIM, (j + 1) * HEAD_DIM), (qh * scale).astype(BF16))
    k = _dot(hb, win_ref[:, o_k:o_k + ATT_W])
    for j, kh in enumerate(_head_norm(k, kg_ref[...])):
        store("k", slice(j * HEAD_DIM, (j + 1) * HEAD_DIM), kh)
    store("v", slice(0, ATT_W), _dot(hb, win_ref[:, o_v:o_v + ATT_W]))


def _p1_kernel(x_ref, n1_ref, win_ref, cw_ref, cb_ref, wg_ref, ba_ref, bx_ref, lam_ref, qg_ref, kg_ref, wbra_ref,
               outa_ref, sgb_ref, q_ref, k_ref, v_ref, xat_ref, hl_ref,
               cxa_ref, ch_ref, ac_ref, uc_ref, hs_ref, *, tm, d_model, first_pos_is_zero):
    t = pl.program_id(1)
    c = d_model

    @pl.when(t == 0)
    def _():
        cxa_ref[...] = jnp.zeros_like(cxa_ref)
        ch_ref[...] = jnp.zeros_like(ch_ref)

    hb = _rmsnorm(x_ref[...], n1_ref[...]).astype(BF16)
    xa = _dot(hb, win_ref[:, 0:c])
    prev8 = cxa_ref[...]
    kw = cw_ref.shape[0]
    xc = cb_ref[...] + _shift_rows(xa, prev8, kw - 1) * cw_ref[0:1, :]
    for i in range(1, kw):
        xc = xc + _shift_rows(xa, prev8, kw - 1 - i) * cw_ref[i:i + 1, :]
    tail = xa[tm - SUBLANES:tm]
    cxa_ref[...] = tail
    xat_ref[...] = tail

    xb = xc.astype(BF16)
    nchunk = c // MXU_DIM
    row = lax.broadcasted_iota(jnp.int32, (tm, MXU_DIM), 0)
    row8 = row & (SUBLANES - 1)
    for j in range(nchunk):
        cs = slice(j * MXU_DIM, (j + 1) * MXU_DIM)
        g = _dot(xb[:, cs], wg_ref[j])
        a, mult, ix = _lru_gates(xc[:, cs], g[:, :MXU_DIM], g[:, MXU_DIM:], ba_ref[:, cs], bx_ref[:, cs],
                                 lam_ref[:, cs])
        if first_pos_is_zero:
            mult = jnp.where((row == 0) & (t == 0), 1.0, mult)
        u = mult * ix
        s = 1
        while s < SUBLANES:
            a_sh = pltpu.roll(a, s, 0)
            u_sh = pltpu.roll(u, s, 0)
            msk = row8 >= s
            u = jnp.where(msk, a * u_sh + u, u)
            a = jnp.where(msk, a * a_sh, a)
            s *= 2
        ac_ref[:, cs] = a
        uc_ref[:, cs] = u

    def grp(gi, hbro):
        r0 = pl.multiple_of(gi * SUBLANES, SUBLANES)
        hh = ac_ref[pl.ds(r0, SUBLANES), :] * hbro + uc_ref[pl.ds(r0, SUBLANES), :]
        hs_ref[pl.ds(r0, SUBLANES), :] = hh
        return jnp.broadcast_to(hh[SUBLANES - 1:SUBLANES, :], hh.shape)

    hbro = lax.fori_loop(0, tm // SUBLANES, grp, ch_ref[...])
    ch_ref[...] = hbro
    hl_ref[...] = hbro

    outs = {"outa": outa_ref, "sgb": sgb_ref, "q": q_ref, "k": k_ref, "v": v_ref}

    def store(name, cols, value):
        outs[name][:, cols] = value

    _stage1_tail(hb, hs_ref[...], win_ref, wbra_ref, qg_ref, kg_ref, store, c)


def _stage1_prompt(x, w, tm):
    b, t, c = x.shape
    assert t % tm == 0 and tm % SUBLANES == 0
    n_in = w["w_in"].shape[1]
    kern = functools.partial(_p1_kernel, tm=tm, d_model=c, first_pos_is_zero=True)
    row_spec = lambda width: pl.BlockSpec((None, tm, width), lambda bi, ti: (bi, ti, 0))
    tail_spec = lambda width: pl.BlockSpec((None, SUBLANES, width), lambda bi, ti: (bi, 0, 0))
    in_specs = [
        row_spec(c),
        _const_spec((1, c)), _const_spec((c, n_in)), _const_spec(w["conv_a_w"].shape), _const_spec((1, c)),
        _const_spec(w["w_gate"].shape), _const_spec((1, c)), _const_spec((1, c)), _const_spec((1, c)),
        _const_spec((N_GROUPS, HEAD_DIM)), _const_spec((N_GROUPS, HEAD_DIM)), _const_spec((c, c)),
    ]
    out_shape = [
        jax.ShapeDtypeStruct((b, t, c), F32), jax.ShapeDtypeStruct((b, t, c), F32),
        jax.ShapeDtypeStruct((b, t, ATT_W), BF16), jax.ShapeDtypeStruct((b, t, ATT_W), F32),
        jax.ShapeDtypeStruct((b, t, ATT_W), F32),
        jax.ShapeDtypeStruct((b, SUBLANES, c), F32), jax.ShapeDtypeStruct((b, SUBLANES, c), F32),
    ]
    out_specs = [row_spec(c), row_spec(c), row_spec(ATT_W), row_spec(ATT_W), row_spec(ATT_W),
                 tail_spec(c), tail_spec(c)]
    scratch = [pltpu.VMEM((SUBLANES, c), F32), pltpu.VMEM((SUBLANES, c), F32),
               pltpu.VMEM((tm, c), F32), pltpu.VMEM((tm, c), F32), pltpu.VMEM((tm, c), F32)]
    return pl.pallas_call(
        kern, grid=(b, t // tm), in_specs=in_specs, out_specs=out_specs, out_shape=out_shape,
        scratch_shapes=scratch, name="stage1_prompt",
        compiler_params=pltpu.CompilerParams(dimension_semantics=("arbitrary", "arbitrary"),
                                             vmem_limit_bytes=VMEM_LIMIT),
    )(x, w["norm1_g"], w["w_in"], w["conv_a_w"], w["conv_a_b"], w["w_gate"], w["lru_ba"], w["lru_bx"],
      w["lru_lambda"], w["q_norm_g"], w["k_norm_g"], w["w_br_a"])


def _s1_kernel(x_ref, h0_ref, cst_ref, n1_ref, win_ref, cw_ref, cb_ref, wg_ref, ba_ref, bx_ref, lam_ref, qg_ref,
               kg_ref, wbra_ref,
               outa_ref, sgb_ref, q_ref, k_ref, v_ref, xat_ref, hl_ref, hs_ref, *, steps, bb, d_model):
    c = d_model
    m = steps * bb
    hb = _rmsnorm(x_ref[...].reshape(m, c), n1_ref[...]).astype(BF16)
    xa = _dot(hb, win_ref[:, 0:c])
    kw = cw_ref.shape[0]
    ext = jnp.concatenate([cst_ref[...].reshape((kw - 1) * bb, c), xa], axis=0)
    xc = cb_ref[...] + ext[0:m] * cw_ref[0:1, :]
    for i in range(1, kw):
        xc = xc + ext[i * bb:i * bb + m] * cw_ref[i:i + 1, :]
    xat_ref[...] = ext[m:].reshape(kw - 1, bb, c)

    xb = xc.astype(BF16)
    h = [None] * (c // MXU_DIM)
    for j in range(c // MXU_DIM):
        cs = slice(j * MXU_DIM, (j + 1) * MXU_DIM)
        g = _dot(xb[:, cs], wg_ref[j])
        a, mult, ix = _lru_gates(xc[:, cs], g[:, :MXU_DIM], g[:, MXU_DIM:], ba_ref[:, cs], bx_ref[:, cs],
                                 lam_ref[:, cs])
        u = mult * ix
        hj = h0_ref[:, cs]
        for s in range(steps):
            hj = a[s * bb:(s + 1) * bb] * hj + u[s * bb:(s + 1) * bb]
            hs_ref[s * bb:(s + 1) * bb, cs] = hj
        hl_ref[:, cs] = hj

    outs = {"outa": outa_ref, "sgb": sgb_ref, "q": q_ref, "k": k_ref, "v": v_ref}

    def store(name, cols, value):
        outs[name][:, :, cols] = value.reshape(steps, bb, value.shape[-1])

    _stage1_tail(hb, hs_ref[...], win_ref, wbra_ref, qg_ref, kg_ref, store, c)


def _stage1_sample(x_tm, h0, conv_state_tm, w, bb):
    steps, nb, c = x_tm.shape
    assert nb % bb == 0 and bb % SUBLANES == 0
    n_in = w["w_in"].shape[1]
    kw = w["conv_a_w"].shape[0]
    kern = functools.partial(_s1_kernel, steps=steps, bb=bb, d_model=c)
    tm_spec = lambda rows, width: pl.BlockSpec((rows, bb, width), lambda bi: (0, bi, 0))
    in_specs = [
        tm_spec(steps, c), pl.BlockSpec((bb, c), lambda bi: (bi, 0)), tm_spec(kw - 1, c),
        _const_spec((1, c)), _const_spec((c, n_in)), _const_spec(w["conv_a_w"].shape), _const_spec((1, c)),
        _const_spec(w["w_gate"].shape), _const_spec((1, c)), _const_spec((1, c)), _const_spec((1, c)),
        _const_spec((N_GROUPS, HEAD_DIM)), _const_spec((N_GROUPS, HEAD_DIM)), _const_spec((c, c)),
    ]
    out_shape = [
        jax.ShapeDtypeStruct((steps, nb, c), F32), jax.ShapeDtypeStruct((steps, nb, c), F32),
        jax.ShapeDtypeStruct((steps, nb, ATT_W), BF16), jax.ShapeDtypeStruct((steps, nb, ATT_W), F32),
        jax.ShapeDtypeStruct((steps, nb, ATT_W), F32),
        jax.ShapeDtypeStruct((kw - 1, nb, c), F32), jax.ShapeDtypeStruct((nb, c), F32),
    ]
    out_specs = [tm_spec(steps, c), tm_spec(steps, c), tm_spec(steps, ATT_W), tm_spec(steps, ATT_W),
                 tm_spec(steps, ATT_W), tm_spec(kw - 1, c), pl.BlockSpec((bb, c), lambda bi: (bi, 0))]
    return pl.pallas_call(
        kern, grid=(nb // bb,), in_specs=in_specs, out_specs=out_specs, out_shape=out_shape,
        scratch_shapes=[pltpu.VMEM((steps * bb, c), F32)], name="stage1_sample",
        compiler_params=pltpu.CompilerParams(dimension_semantics=("arbitrary",), vmem_limit_bytes=VMEM_LIMIT),
    )(x_tm, h0, conv_state_tm, w["norm1_g"], w["w_in"], w["conv_a_w"], w["conv_a_b"], w["w_gate"], w["lru_ba"],
      w["lru_bx"], w["lru_lambda"], w["q_norm_g"], w["k_norm_g"], w["w_br_a"])


def _p2_kernel(q_ref, kp_ref, kc_ref, vp_ref, vc_ref, o_ref, lse_ref, *, tq):
    cblk = pl.program_id(2)
    qi = lax.broadcasted_iota(jnp.int32, (tq, tq), 0)
    kk = lax.broadcasted_iota(jnp.int32, (tq, tq), 1)
    mask_prev = (kk >= qi) & (cblk > 0)
    mask_cur = kk <= qi
    for h in range(HEADS):
        sl = slice(h * HEAD_DIM, (h + 1) * HEAD_DIM)
        q = q_ref[:, sl]
        sp = jnp.where(mask_prev, _dot_nt(q, kp_ref[:, sl].astype(BF16)), NEG_BIG)
        sc = jnp.where(mask_cur, _dot_nt(q, kc_ref[:, sl].astype(BF16)), NEG_BIG)
        m = jnp.maximum(jnp.max(sp, axis=-1, keepdims=True), jnp.max(sc, axis=-1, keepdims=True))
        pp = jnp.exp(sp - m)
        pc = jnp.exp(sc - m)
        den = jnp.sum(pp, axis=-1, keepdims=True) + jnp.sum(pc, axis=-1, keepdims=True)
        o = _dot(pp.astype(BF16), vp_ref[:, sl].astype(BF16)) + _dot(pc.astype(BF16), vc_ref[:, sl].astype(BF16))
        o_ref[:, sl] = o / den
        lse_ref[:, sl] = jnp.broadcast_to(m + jnp.log(den), (tq, HEAD_DIM))


def _stage2_prompt(q, k, v, g, tq):
    window, d = ATT_PATTERNS[g]
    assert window // d + 1 == tq + 1, "the key span of a query block must be the previous + current block"
    b, t, _ = q.shape
    td = t // d
    assert t % d == 0 and td % tq == 0
    nblk = td // tq
    qv, kv, vv = (a.reshape(b, td, d * ATT_W) for a in (q, k, v))
    cur = pl.BlockSpec((None, tq, GROUP_W), lambda bi, r, ci: (bi, ci, r * N_GROUPS + g))
    prev = pl.BlockSpec((None, tq, GROUP_W), lambda bi, r, ci: (bi, jnp.maximum(ci - 1, 0), r * N_GROUPS + g))
    out = pl.BlockSpec((None, tq, GROUP_W), lambda bi, r, ci: (bi, ci, r))
    o, lse = pl.pallas_call(
        functools.partial(_p2_kernel, tq=tq), grid=(b, d, nblk),
        in_specs=[cur, prev, cur, prev, cur], out_specs=[out, out],
        out_shape=[jax.ShapeDtypeStruct((b, td, d * GROUP_W), F32)] * 2, name=f"stage2_prompt_g{g}",
        compiler_params=pltpu.CompilerParams(dimension_semantics=("arbitrary",) * 3),
    )(qv, kv, kv, vv, vv)
    return o.reshape(b, t, GROUP_W), lse.reshape(b, t, GROUP_W)


def _s2_kernel(q_ref, kn_ref, vn_ref, ck0, cv0, ck1, cv1, ck2, cv2, att_ref, *, steps):
    caches = ((ck0, cv0), (ck1, cv1), (ck2, cv2))
    rows = ck0.shape[0]
    tq = lax.broadcasted_iota(jnp.int32, (steps, rows), 0)
    ki = lax.broadcasted_iota(jnp.int32, (steps, rows), 1)
    zpad = jnp.zeros((rows - steps, HEAD_DIM), F32)
    for h in range(HEADS):
        outs, lses = [], []
        for g, (window, d) in enumerate(ATT_PATTERNS):
            assert d & (d - 1) == 0 and window // d == rows
            shift = d.bit_length() - 1
            ck, cv = caches[g]
            c0 = g * GROUP_W + h * HEAD_DIM
            q = q_ref[:, c0:c0 + HEAD_DIM]
            scores, values = [], []
            for r in range(min(d, steps)):
                cc = r * GROUP_W + h * HEAD_DIM
                s = _dot_nt(q, ck[:, cc:cc + HEAD_DIM].astype(BF16))
                valid = ((tq & (d - 1)) == r) & (ki >= (tq >> shift))
                scores.append(jnp.where(valid, s, NEG_BIG))
                values.append(cv[:, cc:cc + HEAD_DIM])
            kn = jnp.concatenate([kn_ref[:, c0:c0 + HEAD_DIM], zpad], axis=0)
            s = _dot_nt(q, kn.astype(BF16))
            valid = (ki <= tq) & (((tq - ki) & (d - 1)) == 0)
            scores.append(jnp.where(valid, s, NEG_BIG))
            values.append(jnp.concatenate([vn_ref[:, c0:c0 + HEAD_DIM], zpad], axis=0))
            m = scores[0].max(axis=-1, keepdims=True)
            for s in scores[1:]:
                m = jnp.maximum(m, s.max(axis=-1, keepdims=True))
            den = jnp.zeros_like(m)
            o = jnp.zeros((steps, HEAD_DIM), F32)
            for s, vals in zip(scores, values):
                p = jnp.exp(s - m)
                den = den + p.sum(axis=-1, keepdims=True)
                o = o + _dot(p.astype(BF16), vals.astype(BF16))
            outs.append(o / den)
            lses.append(m + jnp.log(den))
        mm = functools.reduce(jnp.maximum, lses)
        ws = [jnp.exp(l - mm) for l in lses]
        tot = functools.reduce(lambda x, y: x + y, ws)
        acc = sum(wg * og for wg, og in zip(ws, outs))
        att_ref[:, h * HEAD_DIM:(h + 1) * HEAD_DIM] = acc / tot


def _stage2_sample(q, kn, vn, caches, layer):
    nb, steps, _ = q.shape
    new_spec = pl.BlockSpec((None, steps, ATT_W), lambda bi: (bi, 0, 0))
    in_specs = [new_spec, new_spec, new_spec]
    args = [q, kn, vn]
    for g, (window, d) in enumerate(ATT_PATTERNS):
        for buf in caches[g]:
            depth, nb_c, n_slots = buf.shape[:3]
            assert n_slots == window and PAST_LEN % window == 0 and nb_c == nb
            view = buf.reshape(depth, nb, window // d, d * GROUP_W)
            ncls = min(d, steps)
            in_specs.append(pl.BlockSpec((None, None, window // d, ncls * GROUP_W),
                                         lambda bi: (layer, bi, 0, 0)))
            args.append(view)
    return pl.pallas_call(
        functools.partial(_s2_kernel, steps=steps), grid=(nb,), in_specs=in_specs,
        out_specs=pl.BlockSpec((None, steps, GROUP_W), lambda bi: (bi, 0, 0)),
        out_shape=jax.ShapeDtypeStruct((nb, steps, GROUP_W), F32), name="stage2_sample",
        compiler_params=pltpu.CompilerParams(dimension_semantics=("arbitrary",), vmem_limit_bytes=VMEM_LIMIT),
    )(*args)


def _stage3_core(x, att, outa, sgb, shift_fn, tail_fn, n2_ref, wbrb_ref, wout_ref, wup_ref, cfw_ref, cfb_ref,
                 wdown_ref, d_ff, chunk):
    br_b = _dot(att.astype(BF16), wbrb_ref[...])
    merged = outa + sgb * br_b
    x1 = x + _dot(merged.astype(BF16), wout_ref[...])
    h2 = _rmsnorm(x1, n2_ref[...]).astype(BF16)
    kw = cfw_ref.shape[0]

    def conv(cols):
        u = _dot(h2, wup_ref[:, cols])
        out = cfb_ref[:, cols] + shift_fn(u, cols, kw - 1) * cfw_ref[0:1, cols]
        for i in range(1, kw):
            out = out + shift_fn(u, cols, kw - 1 - i) * cfw_ref[i:i + 1, cols]
        tail_fn(u, cols)
        return out

    acc = x1
    for j in range(d_ff // chunk):
        gf = conv(slice(j * chunk, (j + 1) * chunk))
        vf = conv(slice(d_ff + j * chunk, d_ff + (j + 1) * chunk))
        act = (_gelu(gf) * vf).astype(BF16)
        acc = acc + _dot(act, wdown_ref[j * chunk:(j + 1) * chunk, :])
    return acc


def _merge_groups(o_refs, lse_refs):
    lses = [r[...] for r in lse_refs]
    mm = functools.reduce(jnp.maximum, lses)
    ws = [jnp.exp(l - mm) for l in lses]
    tot = functools.reduce(lambda a, b: a + b, ws)
    acc = sum(wg * r[...] for wg, r in zip(ws, o_refs))
    return acc / tot


def _p3_kernel(x_ref, outa_ref, sgb_ref, o0, o1, o2, l0, l1, l2, n2_ref, wbrb_ref, wout_ref, wup_ref, cfw_ref,
               cfb_ref, wdown_ref, y_ref, upt_ref, cup_ref, *, tm, d_ff, chunk):
    t = pl.program_id(1)

    @pl.when(t == 0)
    def _():
        cup_ref[...] = jnp.zeros_like(cup_ref)

    att = _merge_groups((o0, o1, o2), (l0, l1, l2))

    def shift_fn(u, cols, s):
        return _shift_rows(u, cup_ref[:, cols], s)

    def tail_fn(u, cols):
        tail = u[tm - SUBLANES:tm]
        cup_ref[:, cols] = tail
        upt_ref[:, cols] = tail

    y_ref[...] = _stage3_core(x_ref[...], att, outa_ref[...], sgb_ref[...], shift_fn, tail_fn, n2_ref, wbrb_ref,
                              wout_ref, wup_ref, cfw_ref, cfb_ref, wdown_ref, d_ff, chunk)


def _stage3_prompt(x, outa, sgb, os_, lses, w, tm, chunk):
    b, t, c = x.shape
    d_ff = w["w_down"].shape[0]
    kern = functools.partial(_p3_kernel, tm=tm, d_ff=d_ff, chunk=chunk)
    row_spec = lambda width: pl.BlockSpec((None, tm, width), lambda bi, ti: (bi, ti, 0))
    in_specs = ([row_spec(c)] * 3 + [row_spec(GROUP_W)] * 6 +
                [_const_spec((1, c)), _const_spec(w["w_br_b"].shape), _const_spec((c, c)),
                 _const_spec(w["w_up"].shape), _const_spec(w["conv_f_w"].shape), _const_spec((1, 2 * d_ff)),
                 _const_spec(w["w_down"].shape)])
    return pl.pallas_call(
        kern, grid=(b, t // tm), in_specs=in_specs,
        out_specs=[row_spec(c), pl.BlockSpec((None, SUBLANES, 2 * d_ff), lambda bi, ti: (bi, 0, 0))],
        out_shape=[jax.ShapeDtypeStruct((b, t, c), F32), jax.ShapeDtypeStruct((b, SUBLANES, 2 * d_ff), F32)],
        scratch_shapes=[pltpu.VMEM((SUBLANES, 2 * d_ff), F32)], name="stage3_prompt",
        compiler_params=pltpu.CompilerParams(dimension_semantics=("arbitrary", "arbitrary"),
                                             vmem_limit_bytes=VMEM_LIMIT),
    )(x, outa, sgb, *os_, *lses, w["norm2_g"], w["w_br_b"], w["w_out"], w["w_up"], w["conv_f_w"], w["conv_f_b"],
      w["w_down"])


def _s3_kernel(x_ref, outa_ref, sgb_ref, att_ref, fst_ref, n2_ref, wbrb_ref, wout_ref, wup_ref, cfw_ref, cfb_ref,
               wdown_ref, y_ref, upt_ref, *, steps, bb, d_ff, chunk):
    m = steps * bb
    c = x_ref.shape[-1]
    kw = cfw_ref.shape[0]

    def shift_fn(u, cols, s):
        if s == 0:
            return u
        width = u.shape[-1]
        st = fst_ref[kw - 1 - s:kw - 1, :, cols].reshape(s * bb, width)
        return jnp.concatenate([st, u[0:m - s * bb]], axis=0)

    def tail_fn(u, cols):
        upt_ref[:, :, cols] = u[m - (kw - 1) * bb:m].reshape(kw - 1, bb, u.shape[-1])

    y = _stage3_core(x_ref[...].reshape(m, c), att_ref[...].reshape(m, GROUP_W), outa_ref[...].reshape(m, c),
                     sgb_ref[...].reshape(m, c), shift_fn, tail_fn, n2_ref, wbrb_ref, wout_ref, wup_ref, cfw_ref,
                     cfb_ref, wdown_ref, d_ff, chunk)
    y_ref[...] = y.reshape(steps, bb, c)


def _stage3_sample(x_tm, outa, sgb, att_tm, ffn_state_tm, w, bb, chunk):
    steps, nb, c = x_tm.shape
    d_ff = w["w_down"].shape[0]
    kw = w["conv_f_w"].shape[0]
    kern = functools.partial(_s3_kernel, steps=steps, bb=bb, d_ff=d_ff, chunk=chunk)
    tm_spec = lambda rows, width: pl.BlockSpec((rows, bb, width), lambda bi: (0, bi, 0))
    in_specs = ([tm_spec(steps, c)] * 3 + [tm_spec(steps, GROUP_W), tm_spec(kw - 1, 2 * d_ff)] +
                [_const_spec((1, c)), _const_spec(w["w_br_b"].shape), _const_spec((c, c)),
                 _const_spec(w["w_up"].shape), _const_spec(w["conv_f_w"].shape), _const_spec((1, 2 * d_ff)),
                 _const_spec(w["w_down"].shape)])
    return pl.pallas_call(
        kern, grid=(nb // bb,), in_specs=in_specs,
        out_specs=[tm_spec(steps, c), tm_spec(kw - 1, 2 * d_ff)],
        out_shape=[jax.ShapeDtypeStruct((steps, nb, c), F32), jax.ShapeDtypeStruct((kw - 1, nb, 2 * d_ff), F32)],
        name="stage3_sample",
        compiler_params=pltpu.CompilerParams(dimension_semantics=("arbitrary",), vmem_limit_bytes=VMEM_LIMIT),
    )(x_tm, outa, sgb, att_tm, ffn_state_tm, w["norm2_g"], w["w_br_b"], w["w_out"], w["w_up"], w["conv_f_w"],
      w["conv_f_b"], w["w_down"])


def _block_diag_gate(wa, wx):
    nblk, bs, _ = wa.shape
    per = MXU_DIM // bs
    eye = jnp.eye(per, dtype=wa.dtype)

    def bd(wm):
        w4 = wm.reshape(nblk // per, per, bs, bs)
        return (w4[:, :, :, None, :] * eye[None, :, None, :, None]).reshape(nblk // per, MXU_DIM, MXU_DIM)

    return jnp.concatenate([bd(wa), bd(wx)], axis=-1).astype(BF16)


def _layer_weights(l, norm1_g, w_in, conv_a_w, conv_a_b, lru_wa, lru_ba, lru_wx, lru_bx, lru_lambda, q_norm_g,
                   k_norm_g, w_br_a, w_br_b, w_out, norm2_g, w_up, conv_f_w, conv_f_b, w_down):
    row = lambda a: a[l][None, :]
    return {
        "norm1_g": row(norm1_g), "w_in": w_in[l].astype(BF16), "conv_a_w": conv_a_w[l], "conv_a_b": row(conv_a_b),
        "w_gate": _block_diag_gate(lru_wa[l], lru_wx[l]), "lru_ba": row(lru_ba), "lru_bx": row(lru_bx),
        "lru_lambda": row(lru_lambda), "q_norm_g": q_norm_g[l], "k_norm_g": k_norm_g[l],
        "w_br_a": w_br_a[l].astype(BF16), "w_br_b": w_br_b[l].astype(BF16), "w_out": w_out[l].astype(BF16),
        "norm2_g": row(norm2_g), "w_up": w_up[l].astype(BF16), "conv_f_w": conv_f_w[l], "conv_f_b": row(conv_f_b),
        "w_down": w_down[l].astype(BF16),
    }


PROMPT_TM = 256
SAMPLE_BB = 32
FFN_CHUNK = 512
ATT_TQ = 128


def _prompt_layer(x, w):
    b, t, c = x.shape
    outa, sgb, q, k, v, xa_tail, h_last = _stage1_prompt(x, w, PROMPT_TM)
    os_, lses = [], []
    for g in range(N_GROUPS):
        o, lse = _stage2_prompt(q, k, v, g, ATT_TQ)
        os_.append(o)
        lses.append(lse)
    y, up_tail = _stage3_prompt(x, outa, sgb, os_, lses, w, PROMPT_TM, FFN_CHUNK)
    ka = w["conv_a_w"].shape[0]
    kf = w["conv_f_w"].shape[0]
    states = [h_last[:, 0, :], xa_tail[:, SUBLANES - (ka - 1):, :]]
    for g, (window, _) in enumerate(ATT_PATTERNS):
        keep = min(window, t)
        shift = (t - keep) % keep
        for a in (k, v):
            rows = a[:, t - keep:, g * GROUP_W:(g + 1) * GROUP_W].reshape(b, keep, HEADS, HEAD_DIM)
            states.append(jnp.roll(rows, shift, axis=1) if shift else rows)
    states.append(up_tail[:, SUBLANES - (kf - 1):, :])
    return y, states


def _sample_layer(x, h0, conv_state, caches, ffn_state, w, layer):
    nb, steps, c = x.shape
    tmaj = lambda a: jnp.swapaxes(a, 0, 1)
    x_tm = tmaj(x)
    outa, sgb, q, k, v, xa_tail, h_last = _stage1_sample(x_tm, h0, tmaj(conv_state), w, SAMPLE_BB)
    kb, vb = tmaj(k), tmaj(v)
    att = _stage2_sample(tmaj(q), kb, vb, caches, layer)
    y_tm, up_tail = _stage3_sample(x_tm, outa, sgb, tmaj(att), tmaj(ffn_state), w, SAMPLE_BB, FFN_CHUNK)
    states = [h_last, tmaj(xa_tail)]
    for g in range(N_GROUPS):
        for a in (kb, vb):
            states.append(a[:, :, g * GROUP_W:(g + 1) * GROUP_W].reshape(nb, steps, HEADS, HEAD_DIM))
    states.append(tmaj(up_tail))
    return tmaj(y_tm), states


def kernel(x_prompt, x_sample, state_lru_h, state_lru_conv, cache_k_g0, cache_v_g0, cache_k_g1, cache_v_g1, cache_k_g2, cache_v_g2, state_ffn_conv, norm1_g, w_in, conv_a_w, conv_a_b, lru_wa, lru_ba, lru_wx, lru_bx, lru_lambda, q_norm_g, k_norm_g, w_br_a, w_br_b, w_out, norm2_g, w_up, conv_f_w, conv_f_b, w_down):
    depth = w_in.shape[0]
    caches = [(cache_k_g0, cache_v_g0), (cache_k_g1, cache_v_g1), (cache_k_g2, cache_v_g2)]
    yp, ys = x_prompt, x_sample
    p_states = [[] for _ in range(9)]
    s_states = [[] for _ in range(9)]
    for l in range(depth):
        w = _layer_weights(l, norm1_g, w_in, conv_a_w, conv_a_b, lru_wa, lru_ba, lru_wx, lru_bx, lru_lambda,
                           q_norm_g, k_norm_g, w_br_a, w_br_b, w_out, norm2_g, w_up, conv_f_w, conv_f_b, w_down)
        yp, st_p = _prompt_layer(yp, w)
        ys, st_s = _sample_layer(ys, state_lru_h[l], state_lru_conv[l], caches, state_ffn_conv[l], w, l)
        for i in range(9):
            p_states[i].append(st_p[i])
            s_states[i].append(st_s[i])
    p_out = [jnp.stack(s, axis=0) for s in p_states]
    s_out = [jnp.stack(s, axis=0) for s in s_states]
    return (yp, ys, *p_out, *s_out)
```
